```python
import math
import jax, jax.numpy as jnp
from jax import lax
import numpy as np

D_MODEL = 2048
BATCH = 2
SEQ = 4096
DEPTH = 2

D_MIX = D_MODEL
RW_WIDTH = D_MIX // 2
RW_HEAD = 64
RW_HEADS = RW_WIDTH // RW_HEAD
RW_DECAY_LORA = 64
RW_AAA_LORA = 64
RW_MV_LORA = 32
RW_GATE_LORA = 160
RW_GN_EPS = 64e-5
RW_COLS_FIRST = 3 * RW_WIDTH + RW_DECAY_LORA + RW_AAA_LORA + RW_GATE_LORA
RW_COLS_REST = RW_COLS_FIRST + RW_MV_LORA
DA_WIDTH = D_MIX - RW_WIDTH
DA_HD = 64
DA_HEADS = DA_WIDTH // (2 * DA_HD)
DA_COLS = 3 * DA_WIDTH
ROPE_DIMS = DA_HD // 4
ROPE_THETA = 500000.0
Q_BLOCK = 128
N_IN_FIRST = RW_COLS_FIRST + DA_COLS
N_IN_REST = RW_COLS_REST + DA_COLS
N_GROUPS = 4
EXP_PER_GROUP = 8
N_EXPERTS = N_GROUPS * EXP_PER_GROUP
TOP_K_IN_GROUP = 2
D_EXPERT = D_MODEL // 4
PLE_DIM = 256
EPS = 1e-6
NEG_INF = -1e30

kernel_name = "hymba_rwkv7_diffattn_hiermoe"


def rms_norm(x, g, eps=EPS):
    xf = x.astype(jnp.float32)
    y = xf * lax.rsqrt(jnp.mean(xf * xf, axis=-1, keepdims=True) + eps)
    return (y * g.astype(jnp.float32)).astype(x.dtype)


def rope_partial(x, cos, sin):
    half = ROPE_DIMS // 2
    c = cos[None, :, None, None, :]
    s = sin[None, :, None, None, :]
    x1 = x[..., :half]
    x2 = x[..., half:ROPE_DIMS]
    return jnp.concatenate([x1 * c - x2 * s, x2 * c + x1 * s, x[..., ROPE_DIMS:]], axis=-1)


def wkv7_scan(r, w, k, v, a, b):
    B, T, H, N = r.shape

    def step(S, inp):
        r_t, w_t, k_t, v_t, a_t, b_t = inp
        sa = jnp.einsum('bhvk,bhk->bhv', S, a_t)
        S = S * w_t[:, :, None, :] + sa[..., None] * b_t[:, :, None, :] + v_t[..., None] * k_t[:, :, None, :]
        return S, jnp.einsum('bhvk,bhk->bhv', S, r_t)

    xs = tuple(jnp.moveaxis(t, 1, 0) for t in (r, w, k, v, a, b))
    S0 = jnp.zeros((B, H, N, N), jnp.float32)
    _, ys = lax.scan(step, S0, xs)
    return jnp.moveaxis(ys, 0, 1)


def rwkv7_time_mix(proj, mu, w0, w2, a0, a2, g2, k_k, k_a, r_k, gn_g, gn_b, v_first, v0, v2):
    B, T, _ = proj.shape
    C, H, N = RW_WIDTH, RW_HEADS, RW_HEAD
    f32 = jnp.float32
    pf = proj.astype(f32)
    prev = jnp.pad(pf, ((0, 0), (1, 0), (0, 0)))[:, :-1, :]
    xs = pf + (prev - pf) * mu.astype(f32)
    o = 3 * C
    r, k, v = xs[..., :C], xs[..., C:2 * C], xs[..., 2 * C:o]
    wl = xs[..., o:o + RW_DECAY_LORA]; o += RW_DECAY_LORA
    al = xs[..., o:o + RW_AAA_LORA]; o += RW_AAA_LORA
    gl = xs[..., o:o + RW_GATE_LORA]; o += RW_GATE_LORA
    w = -jax.nn.softplus(-(w0.astype(f32) + jnp.tanh(wl) @ w2.astype(f32))) - 0.5
    decay = jnp.exp(-jnp.exp(w))
    if v_first is None:
        v_first = v
    else:
        vl = xs[..., o:o + RW_MV_LORA]
        v = v + (v_first - v) * jax.nn.sigmoid(v0.astype(f32) + vl @ v2.astype(f32))
    a = jax.nn.sigmoid(a0.astype(f32) + al @ a2.astype(f32))
    g = jax.nn.sigmoid(gl) @ g2.astype(f32)
    kk = (k * k_k.astype(f32)).reshape(B, T, H, N)
    kk = kk / jnp.maximum(jnp.sqrt(jnp.sum(kk * kk, axis=-1, keepdims=True)), 1e-12)
    k = k * (1.0 + (a - 1.0) * k_a.astype(f32))
    rh, kh, vh = r.reshape(B, T, H, N), k.reshape(B, T, H, N), v.reshape(B, T, H, N)
    ah = a.reshape(B, T, H, N)
    y = wkv7_scan(rh, decay.reshape(B, T, H, N), kh, vh, -kk, kk * ah)
    mean = jnp.mean(y, axis=-1, keepdims=True)
    var = jnp.mean(jnp.square(y - mean), axis=-1, keepdims=True)
    yn = ((y - mean) * lax.rsqrt(var + RW_GN_EPS)).reshape(B, T, C) * gn_g.astype(f32) + gn_b.astype(f32)
    bonus = (jnp.sum(rh * kh * r_k.astype(f32), axis=-1, keepdims=True) * vh).reshape(B, T, C)
    out = (yn + bonus) * g
    return out.astype(proj.dtype), v_first


def diff_attention(q, k, v, q_g, k_g, lq1, lk1, lq2, lk2, subln_g, lambda_init, cos, sin):
    B, T, H, _, hd = q.shape
    q = rope_partial(rms_norm(q, q_g), cos, sin)
    k = rope_partial(rms_norm(k, k_g), cos, sin)
    f32 = jnp.float32
    lam = (jnp.exp(jnp.sum(lq1.astype(f32) * lk1.astype(f32))) -
           jnp.exp(jnp.sum(lq2.astype(f32) * lk2.astype(f32))) + lambda_init)
    scale = hd ** -0.5
    nb = T // Q_BLOCK
    q_blocks = jnp.moveaxis(q.reshape(B, nb, Q_BLOCK, H, 2, hd), 1, 0)
    starts = jnp.arange(nb, dtype=jnp.int32) * Q_BLOCK
    k_pos = jnp.arange(T, dtype=jnp.int32)

    def block_fn(args):
        qb, start = args
        s = jnp.einsum('bqhcd,bkhcd->bhcqk', qb, k, preferred_element_type=f32) * scale
        q_pos = start + jnp.arange(Q_BLOCK, dtype=jnp.int32)
        mask = k_pos[None, :] <= q_pos[:, None]
        pr = jax.nn.softmax(jnp.where(mask, s, NEG_INF), axis=-1)
        attn = pr[:, :, 0] - lam * pr[:, :, 1]
        return jnp.einsum('bhqk,bkhe->bqhe', attn.astype(v.dtype), v)

    out = lax.map(block_fn, (q_blocks, starts))
    out = jnp.moveaxis(out, 0, 1).reshape(B, T, H, 2 * hd)
    out = rms_norm(out, subln_g) * (1.0 - lambda_init)
    return out.reshape(B, T, H * 2 * hd)


def hier_moe(x, wr_group, br_group, wr_expert, br_expert, w_gate, w_up, w_down):
    B, T, D = x.shape
    n = B * T
    xt = x.reshape(n, D)
    f32 = jnp.float32
    lg = (xt @ wr_group + br_group).astype(f32)
    pg = jax.nn.softmax(lg, axis=-1)
    gp, grp = lax.top_k(pg, 1)
    le = (xt @ wr_expert + br_expert).astype(f32).reshape(n, N_GROUPS, EXP_PER_GROUP)
    le_sel = jnp.take_along_axis(le, grp[:, :, None], axis=1)[:, 0]
    pe = jax.nn.softmax(le_sel, axis=-1)
    top_p, top_i = lax.top_k(pe, TOP_K_IN_GROUP)
    top_p = top_p / jnp.sum(top_p, axis=-1, keepdims=True)
    expert_idx = grp * EXP_PER_GROUP + top_i
    weight = gp * top_p
    combine = jnp.sum(jax.nn.one_hot(expert_idx, N_EXPERTS, dtype=f32) * weight[..., None], axis=1)
    combine = combine.astype(x.dtype)
    y = jnp.zeros((n, D), x.dtype)
    for gi in range(N_GROUPS):
        sl = slice(gi * EXP_PER_GROUP, (gi + 1) * EXP_PER_GROUP)
        hg = jax.nn.silu(jnp.einsum('nd,edf->enf', xt, w_gate[sl])) * jnp.einsum('nd,edf->enf', xt, w_up[sl])
        hg = hg * combine[:, sl].T[:, :, None]
        y = y + jnp.einsum('enf,efd->nd', hg, w_down[sl])
    return y.reshape(B, T, D)


def setup_inputs(seed: int = 0) -> dict:
    key = jax.random.key(seed)
    ks = iter(jax.random.split(key, 48))
    f32 = jnp.float32

    def nrm(shape, scale):
        return jax.random.normal(next(ks), shape, f32) * scale

    def gain(shape):
        return 1.0 + nrm(shape, 0.02)

    L, Lr = DEPTH, DEPTH - 1
    C = RW_WIDTH
    return {
        "x": nrm((BATCH, SEQ, D_MODEL), 1.0),
        "p": nrm((DEPTH, BATCH, SEQ, PLE_DIM), 1.0),
        "ln1_g": gain((L, D_MODEL)),
        "ln2_g": gain((L, D_MODEL)),
        "w_in_first": nrm((D_MODEL, N_IN_FIRST), D_MODEL ** -0.5),
        "w_in_rest": nrm((Lr, D_MODEL, N_IN_REST), D_MODEL ** -0.5),
        "mu_first": jax.random.uniform(next(ks), (RW_COLS_FIRST,), f32),
        "mu_rest": jax.random.uniform(next(ks), (Lr, RW_COLS_REST), f32),
        "rw_w0": jax.random.uniform(next(ks), (L, C), f32, -6.0, -1.0),
        "rw_w2": nrm((L, RW_DECAY_LORA, C), 0.5 * RW_DECAY_LORA ** -0.5),
        "rw_a0": nrm((L, C), 0.1),
        "rw_a2": nrm((L, RW_AAA_LORA, C), RW_AAA_LORA ** -0.5),
        "rw_g2": nrm((L, RW_GATE_LORA, C), RW_GATE_LORA ** -0.5),
        "rw_v0": nrm((Lr, C), 0.1),
        "rw_v2": nrm((Lr, RW_MV_LORA, C), RW_MV_LORA ** -0.5),
        "rw_k_k": 0.85 + nrm((L, C), 0.02),
        "rw_k_a": gain((L, C)),
        "rw_r_k": nrm((L, RW_HEADS, RW_HEAD), 0.1),
        "rw_gn_g": gain((L, C)),
        "rw_gn_b": nrm((L, C), 0.01),
        "da_q_g": gain((L, DA_HD)),
        "da_k_g": gain((L, DA_HD)),
        "da_lq1": nrm((L, DA_HD), 0.1),
        "da_lk1": nrm((L, DA_HD), 0.1),
        "da_lq2": nrm((L, DA_HD), 0.1),
        "da_lk2": nrm((L, DA_HD), 0.1),
        "da_subln_g": gain((L, 2 * DA_HD)),
        "w_out": nrm((L, D_MIX, D_MODEL), D_MIX ** -0.5),
        "moe_wr_group": nrm((L, D_MODEL, N_GROUPS), D_MODEL ** -0.5),
        "moe_br_group": nrm((L, N_GROUPS), 0.01),
        "moe_wr_expert": nrm((L, D_MODEL, N_EXPERTS), D_MODEL ** -0.5),
        "moe_br_expert": nrm((L, N_EXPERTS), 0.01),
        "moe_w_gate": nrm((L, N_EXPERTS, D_MODEL, D_EXPERT), D_MODEL ** -0.5),
        "moe_w_up": nrm((L, N_EXPERTS, D_MODEL, D_EXPERT), D_MODEL ** -0.5),
        "moe_w_down": nrm((L, N_EXPERTS, D_EXPERT, D_MODEL), D_EXPERT ** -0.5),
        "ple_w_gate": nrm((L, D_MODEL, D_MODEL), D_MODEL ** -0.5),
        "ple_b_gate": nrm((L, D_MODEL), 0.01),
        "ple_w_proj": nrm((L, PLE_DIM, D_MODEL), PLE_DIM ** -0.5),
    }


def reference(x, p, ln1_g, ln2_g, w_in_first, w_in_rest, mu_first, mu_rest, rw_w0, rw_w2, rw_a0, rw_a2,
              rw_g2, rw_v0, rw_v2, rw_k_k, rw_k_a, rw_r_k, rw_gn_g, rw_gn_b, da_q_g, da_k_g, da_lq1, da_lk1,
              da_lq2, da_lk2, da_subln_g, w_out, moe_wr_group, moe_br_group, moe_wr_expert, moe_br_expert,
              moe_w_gate, moe_w_up, moe_w_down, ple_w_gate, ple_b_gate, ple_w_proj):
    B, T, _ = x.shape
    pos = jnp.arange(T, dtype=jnp.float32)
    inv_freq = ROPE_THETA ** (-(jnp.arange(0, ROPE_DIMS, 2, dtype=jnp.float32) / ROPE_DIMS))
    ang = pos[:, None] * inv_freq[None, :]
    cos = jnp.cos(ang).astype(x.dtype)
    sin = jnp.sin(ang).astype(x.dtype)

    h = x
    v_first = None
    for i in range(DEPTH):
        xn = rms_norm(h, ln1_g[i])
        if i == 0:
            proj = xn @ w_in_first
            n_rw, mu, v0, v2 = RW_COLS_FIRST, mu_first, None, None
        else:
            proj = xn @ w_in_rest[i - 1]
            n_rw, mu, v0, v2 = RW_COLS_REST, mu_rest[i - 1], rw_v0[i - 1], rw_v2[i - 1]
        y_rw, v_first = rwkv7_time_mix(proj[..., :n_rw], mu, rw_w0[i], rw_w2[i], rw_a0[i], rw_a2[i],
                                       rw_g2[i], rw_k_k[i], rw_k_a[i], rw_r_k[i], rw_gn_g[i], rw_gn_b[i],
                                       v_first, v0, v2)
        da = proj[..., n_rw:]
        q = da[..., :DA_WIDTH].reshape(B, T, DA_HEADS, 2, DA_HD)
        k = da[..., DA_WIDTH:2 * DA_WIDTH].reshape(B, T, DA_HEADS, 2, DA_HD)
        v = da[..., 2 * DA_WIDTH:].reshape(B, T, DA_HEADS, 2 * DA_HD)
        lambda_init = 0.8 - 0.6 * math.exp(-0.3 * i)
        y_da = diff_attention(q, k, v, da_q_g[i], da_k_g[i], da_lq1[i], da_lk1[i], da_lq2[i], da_lk2[i],
                              da_subln_g[i], lambda_init, cos, sin)
        h = h + jnp.concatenate([y_rw, y_da], axis=-1) @ w_out[i]
        h = h + hier_moe(rms_norm(h, ln2_g[i]), moe_wr_group[i], moe_br_group[i], moe_wr_expert[i],
                         moe_br_expert[i], moe_w_gate[i], moe_w_up[i], moe_w_down[i])
        h = h + jax.nn.sigmoid(h @ ple_w_gate[i] + ple_b_gate[i]) * (p[i] @ ple_w_proj[i])
    return h
```

```python
import functools
import math

import jax
import jax.numpy as jnp
from jax import lax
from jax.experimental import pallas as pl
from jax.experimental.pallas import tpu as pltpu

F32 = jnp.float32
BF16 = jnp.bfloat16
HIGHEST = lax.Precision.HIGHEST

D_MODEL = 2048
RW_WIDTH = 1024
RW_HEAD = 64
RW_HEADS = 16
RW_DECAY_LORA = 64
RW_AAA_LORA = 64
RW_MV_LORA = 32
RW_GATE_LORA = 160
RW_GN_EPS = 64e-5
LORA_PAD = 384
RW_COLS_PAD = 3 * RW_WIDTH + LORA_PAD
DA_WIDTH = 1024
DA_HD = 64
DA_HEADS = 8
ROPE_DIMS = 16
ROPE_THETA = 500000.0
N_GROUPS = 4
EXP_PER_GROUP = 8
N_EXPERTS = 32
D_EXPERT = 512
PLE_DIM = 256
EPS = 1e-6
NEG_INF = -1e30

LANES = 128
VMEM_LIMIT = 56 * 1024 * 1024

WKV_CHUNK = 64
MOE_TILE = 256


def _cparams(sem):
    return pltpu.CompilerParams(dimension_semantics=sem, vmem_limit_bytes=VMEM_LIMIT)


def _sigmoid(x):
    return 1.0 / (1.0 + jnp.exp(-x))


def _block_ones(width):
    shift = int(math.log2(width))
    ri = lax.shift_right_logical(lax.broadcasted_iota(jnp.int32, (LANES, LANES), 0), shift)
    ci = lax.shift_right_logical(lax.broadcasted_iota(jnp.int32, (LANES, LANES), 1), shift)
    return (ri == ci).astype(F32)


def _dot(a, b, precision=None):
    return jnp.dot(a, b, preferred_element_type=F32, precision=precision)


def _dot_nt(a, b, precision=None):
    return lax.dot_general(a, b, (((1,), (1,)), ((), ())),
                           preferred_element_type=F32, precision=precision)


def _dot_tn(a, b, precision=None):
    return lax.dot_general(a, b, (((0,), (0,)), ((), ())),
                           preferred_element_type=F32, precision=precision)


def _norm_matmul_kernel(x_ref, g_ref, w_ref, o_ref, xn_ref):
    @pl.when(pl.program_id(1) == 0)
    def _():
        x = x_ref[...]
        ms = jnp.mean(x * x, axis=-1, keepdims=True)
        xn_ref[...] = (x * lax.rsqrt(ms + EPS) * g_ref[...]).astype(BF16)

    o_ref[...] = _dot(xn_ref[...], w_ref[...])


def norm_matmul(x, g, w, tm, tn):
    m, k = x.shape
    n = w.shape[1]
    return pl.pallas_call(
        _norm_matmul_kernel,
        grid=(m // tm, n // tn),
        in_specs=[
            pl.BlockSpec((tm, k), lambda i, j: (i, 0)),
            pl.BlockSpec((1, k), lambda i, j: (0, 0)),
            pl.BlockSpec((k, tn), lambda i, j: (0, j)),
        ],
        out_specs=pl.BlockSpec((tm, tn), lambda i, j: (i, j)),
        out_shape=jax.ShapeDtypeStruct((m, n), F32),
        scratch_shapes=[pltpu.VMEM((tm, k), BF16)],
        compiler_params=_cparams(("parallel", "arbitrary")),
        name="norm_matmul",
    )(x, g.reshape(1, k), w)


def _rwkv_prep_kernel(*refs, tiles_per_seq, has_vfirst):
    if has_vfirst:
        (x_ref, pv_ref, mu_ref, vec_ref, w2_ref, a2_ref, g2_ref, v2_ref, vf_ref,
         r_ref, lw_ref, k_ref, v_ref, kk_ref, a_ref, g_ref) = refs
    else:
        (x_ref, pv_ref, mu_ref, vec_ref, w2_ref, a2_ref, g2_ref,
         r_ref, lw_ref, k_ref, v_ref, kk_ref, a_ref, g_ref) = refs
    first = (pl.program_id(0) % tiles_per_seq) == 0
    c = RW_WIDTH

    def shifted(c0, c1):
        pf = x_ref[:, c0:c1]
        prow = jnp.where(first, 0.0, pv_ref[7:8, c0:c1])
        rolled = pltpu.roll(pf, 1, 0)
        rid = lax.broadcasted_iota(jnp.int32, pf.shape, 0)
        prev = jnp.where(rid == 0, prow, rolled)
        return pf + (prev - pf) * mu_ref[:, c0:c1]

    w0 = vec_ref[0:1, :]
    a0 = vec_ref[1:2, :]
    v0 = vec_ref[2:3, :]
    k_k = vec_ref[3:4, :]
    k_a = vec_ref[4:5, :]

    lo = shifted(3 * c, 3 * c + LANES)
    hi = shifted(3 * c + LANES, 3 * c + LORA_PAD)

    z = -(w0 + _dot(jnp.tanh(lo), w2_ref[...], HIGHEST))
    softplus = jnp.maximum(z, 0.0) + jnp.log(1.0 + jnp.exp(-jnp.abs(z)))
    lw_ref[...] = -jnp.exp(-softplus - 0.5)

    a = _sigmoid(a0 + _dot(lo, a2_ref[...], HIGHEST))
    a_ref[...] = a
    g_ref[...] = _dot(_sigmoid(hi), g2_ref[...], HIGHEST)

    r_ref[...] = shifted(0, c)
    k = shifted(c, 2 * c)
    v = shifted(2 * c, 3 * c)
    if has_vfirst:
        mix = _sigmoid(v0 + _dot(hi, v2_ref[...], HIGHEST))
        v = v + (vf_ref[...] - v) * mix
    v_ref[...] = v

    kk = k * k_k
    kk2 = kk * kk
    head_ones = _block_ones(RW_HEAD)
    for blk in range(c // LANES):
        sl = slice(blk * LANES, (blk + 1) * LANES)
        ss = _dot(kk2[:, sl], head_ones, HIGHEST)
        kk_ref[:, sl] = kk[:, sl] / jnp.maximum(jnp.sqrt(ss), 1e-12)
    k_ref[...] = k * (1.0 + (a - 1.0) * k_a)


def rwkv_prep(proj, mu, vec, w2p, a2p, g2p, v2p, v_first, seq, tt=256):
    n = proj.shape[0]
    c = RW_WIDTH
    has_vfirst = v_first is not None
    kern = functools.partial(_rwkv_prep_kernel, tiles_per_seq=seq // tt, has_vfirst=has_vfirst)
    row = lambda i: (i, 0)
    const = lambda i: (0, 0)
    in_specs = [
        pl.BlockSpec((tt, RW_COLS_PAD), row),
        pl.BlockSpec((8, RW_COLS_PAD), lambda i: (jnp.maximum(i * (tt // 8) - 1, 0), 0)),
        pl.BlockSpec((1, RW_COLS_PAD), const),
        pl.BlockSpec((8, c), const),
        pl.BlockSpec((LANES, c), const),
        pl.BlockSpec((LANES, c), const),
        pl.BlockSpec((2 * LANES, c), const),
    ]
    args = [proj, proj, mu, vec, w2p, a2p, g2p]
    if has_vfirst:
        in_specs += [pl.BlockSpec((2 * LANES, c), const), pl.BlockSpec((tt, c), row)]
        args += [v2p, v_first]
    out = jax.ShapeDtypeStruct((n, c), F32)
    return pl.pallas_call(
        kern,
        grid=(n // tt,),
        in_specs=in_specs,
        out_specs=[pl.BlockSpec((tt, c), row)] * 7,
        out_shape=[out] * 7,
        compiler_params=_cparams(("parallel",)),
        name="rwkv_prep",
    )(*args)


def _wkv_kernel(r_ref, lw_ref, k_ref, v_ref, kk_ref, a_ref, g_ref, par_ref, o_ref, s_ref, y_ref, *, prec):
    L = WKV_CHUNK
    N = RW_HEAD

    @pl.when(pl.program_id(1) == 0)
    def _():
        s_ref[...] = jnp.zeros_like(s_ref)

    ri = lax.broadcasted_iota(jnp.int32, (L, L), 0)
    ci = lax.broadcasted_iota(jnp.int32, (L, L), 1)
    lower = ri >= ci
    strict = ri > ci
    eye = (ri == ci).astype(F32)

    lw = lw_ref[...]
    cum = _dot(lower.astype(F32), lw, HIGHEST)
    g_inc = jnp.exp(cum)
    g_inv = jnp.exp(-cum)
    g_prev = jnp.exp(cum - lw)
    r = r_ref[...]
    k = k_ref[...]
    v = v_ref[...]
    kk = kk_ref[...]
    rt = r * g_inc
    kt = k * g_inv
    bt = kk * a_ref[...] * g_inv
    at = -kk * g_prev
    g_last = g_inc[L - 1:L, :]
    rk = par_ref[0:1, :]
    gn_g = par_ref[1:2, :]
    gn_b = par_ref[2:3, :]
    gate = g_ref[...]

    for h in range(RW_HEADS):
        sl = slice(h * N, (h + 1) * N)
        at_h, rt_h, bt_h, kt_h, v_h = at[:, sl], rt[:, sl], bt[:, sl], kt[:, sl], v[:, sl]
        ar = jnp.concatenate([at_h, rt_h], axis=0)
        g1 = _dot_nt(ar, bt_h, prec)
        g2 = _dot_nt(ar, kt_h, prec)
        a_ab = jnp.where(strict, g1[:L], 0.0)
        q_b = jnp.where(lower, g1[L:], 0.0)
        a_ak = jnp.where(strict, g2[:L], 0.0)
        q_k = jnp.where(lower, g2[L:], 0.0)
        inv = eye + a_ab
        p = a_ab
        for _ in range(int(math.log2(L)) - 1):
            p = _dot(p, p, prec)
            inv = inv + _dot(inv, p, prec)
        wm = _dot(inv, at_h, prec)
        u0 = _dot(inv, _dot(a_ak, v_h, prec), prec)
        s = s_ref[h]
        u = _dot_nt(wm, s, prec) + u0
        y = _dot_nt(rt_h, s, prec) + _dot(q_b, u, prec) + _dot(q_k, v_h, prec)
        s_ref[h] = (s + _dot_tn(u, bt_h, prec) + _dot_tn(v_h, kt_h, prec)) * g_last[:, sl]

        mean = jnp.mean(y, axis=-1, keepdims=True)
        yc = y - mean
        var = jnp.mean(yc * yc, axis=-1, keepdims=True)
        yn = yc * lax.rsqrt(var + RW_GN_EPS) * gn_g[:, sl] + gn_b[:, sl]
        bonus = jnp.sum(r[:, sl] * k[:, sl] * rk[:, sl], axis=-1, keepdims=True) * v_h
        y_ref[:, sl] = (yn + bonus) * gate[:, sl]
    o_ref[...] = y_ref[...].astype(o_ref.dtype)


def wkv(r, lw, k, v, kk, a, g, par, batch, seq, prec=HIGHEST):
    n, c = r.shape
    L = WKV_CHUNK
    nc = seq // L
    blk = pl.BlockSpec((L, c), lambda b, j: (b * nc + j, 0))
    return pl.pallas_call(
        functools.partial(_wkv_kernel, prec=prec),
        grid=(batch, nc),
        in_specs=[blk] * 7 + [pl.BlockSpec((8, c), lambda b, j: (0, 0))],
        out_specs=blk,
        out_shape=jax.ShapeDtypeStruct((n, c), BF16),
        scratch_shapes=[pltpu.VMEM((RW_HEADS, RW_HEAD, RW_HEAD), F32), pltpu.VMEM((L, c), F32)],
        compiler_params=_cparams(("parallel", "arbitrary")),
        name="wkv",
    )(r, lw, k, v, kk, a, g, par)


def _attn_prep_kernel(x_ref, cos_ref, sa_ref, sb_ref, gain_ref, q_ref, k_ref, v_ref):
    half_ones = _block_ones(DA_HD)
    cos = cos_ref[...]
    sa = sa_ref[...]
    sb = sb_ref[...]
    half = ROPE_DIMS // 2

    def norm_rope(x, gain):
        ms = _dot(x * x, half_ones, HIGHEST) * (1.0 / DA_HD)
        xn = x * lax.rsqrt(ms + EPS) * gain
        return xn * cos + pltpu.roll(xn, LANES - half, 1) * sa + pltpu.roll(xn, half, 1) * sb

    for h in range(DA_HEADS):
        sl = slice(h * LANES, (h + 1) * LANES)
        q = norm_rope(x_ref[:, sl], gain_ref[0:1, :])
        q_ref[:, sl] = (q * (DA_HD ** -0.5)).astype(BF16)
        ksl = slice(DA_WIDTH + h * LANES, DA_WIDTH + (h + 1) * LANES)
        k_ref[:, sl] = norm_rope(x_ref[:, ksl], gain_ref[1:2, :]).astype(BF16)
    v_ref[...] = x_ref[:, 2 * DA_WIDTH:].astype(BF16)


def attn_prep(proj_da, cos_t, sa_t, sb_t, gains, seq, tt=256):
    n = proj_da.shape[0]
    tps = seq // tt
    row = lambda i: (i, 0)
    tab = pl.BlockSpec((tt, LANES), lambda i: (i % tps, 0))
    out = jax.ShapeDtypeStruct((n, DA_WIDTH), BF16)
    return pl.pallas_call(
        _attn_prep_kernel,
        grid=(n // tt,),
        in_specs=[pl.BlockSpec((tt, 3 * DA_WIDTH), row), tab, tab, tab,
                  pl.BlockSpec((8, LANES), lambda i: (0, 0))],
        out_specs=[pl.BlockSpec((tt, DA_WIDTH), row)] * 3,
        out_shape=[out] * 3,
        compiler_params=_cparams(("parallel",)),
        name="attn_prep",
    )(proj_da, cos_t, sa_t, sb_t, gains)


def _flash_kernel(q_ref, k_ref, v_ref, par_ref, o_ref, m_ref, l_ref, acc_ref, *, lambda_init, tq):
    i = pl.program_id(2)
    j = pl.program_id(3)

    @pl.when(j == 0)
    def _():
        m_ref[...] = jnp.full_like(m_ref, NEG_INF)
        l_ref[...] = jnp.zeros_like(l_ref)
        acc_ref[...] = jnp.zeros_like(acc_ref)

    def step(masked):
        q = q_ref[...]
        k = k_ref[...]
        v = v_ref[...]
        lane = lax.broadcasted_iota(jnp.int32, q.shape, 1)
        zero = jnp.zeros_like(q)
        halves = (jnp.where(lane < DA_HD, q, zero), jnp.where(lane >= DA_HD, q, zero))
        if masked:
            rows = lax.broadcasted_iota(jnp.int32, (tq, tq), 0)
            cols = lax.broadcasted_iota(jnp.int32, (tq, tq), 1)
            keep = cols <= rows
        for c, qc in enumerate(halves):
            s = _dot_nt(qc, k)
            if masked:
                s = jnp.where(keep, s, NEG_INF)
            m_prev = m_ref[c]
            m_new = jnp.maximum(m_prev, jnp.max(s, axis=-1, keepdims=True))
            alpha = jnp.exp(m_prev - m_new)
            p = jnp.exp(s - m_new)
            l_ref[c] = alpha * l_ref[c] + jnp.sum(p, axis=-1, keepdims=True)
            acc_ref[c] = alpha * acc_ref[c] + _dot(p.astype(BF16), v)
            m_ref[c] = m_new

    @pl.when(j < i)
    def _():
        step(False)

    @pl.when(j == i)
    def _():
        step(True)
        lq1, lk1 = par_ref[0:1, :], par_ref[1:2, :]
        lq2, lk2 = par_ref[2:3, :], par_ref[3:4, :]
        lam = (jnp.exp(jnp.sum(lq1 * lk1, axis=-1, keepdims=True))
               - jnp.exp(jnp.sum(lq2 * lk2, axis=-1, keepdims=True)) + lambda_init)
        o = acc_ref[0] / l_ref[0] - lam * (acc_ref[1] / l_ref[1])
        ms = jnp.mean(o * o, axis=-1, keepdims=True)
        o = o * lax.rsqrt(ms + EPS) * par_ref[4:5, :] * (1.0 - lambda_init)
        o_ref[...] = o.astype(o_ref.dtype)


def flash_diff_attention(q, k, v, par, batch, seq, lambda_init, tq=512):
    n = q.shape[0]
    nq = seq // tq
    qspec = pl.BlockSpec((tq, LANES), lambda b, h, i, j: (b * nq + i, h))
    kspec = pl.BlockSpec((tq, LANES), lambda b, h, i, j: (b * nq + jnp.minimum(j, i), h))
    return pl.pallas_call(
        functools.partial(_flash_kernel, lambda_init=lambda_init, tq=tq),
        grid=(batch, DA_HEADS, nq, nq),
        in_specs=[qspec, kspec, kspec, pl.BlockSpec((8, LANES), lambda b, h, i, j: (0, 0))],
        out_specs=qspec,
        out_shape=jax.ShapeDtypeStruct((n, DA_WIDTH), BF16),
        scratch_shapes=[pltpu.VMEM((2, tq, 1), F32), pltpu.VMEM((2, tq, 1), F32),
                        pltpu.VMEM((2, tq, LANES), F32)],
        compiler_params=_cparams(("parallel", "parallel", "parallel", "arbitrary")),
        name="flash_diff_attn",
    )(q, k, v, par)


def _outproj_router_kernel(h_ref, yr_ref, yd_ref, wa_ref, wb_ref, g_ref, wr_ref, br_ref,
                           h1_ref, xn_ref, ei_ref, ew_ref):
    h1 = h_ref[...] + _dot(yr_ref[...], wa_ref[...]) + _dot(yd_ref[...], wb_ref[...])
    h1_ref[...] = h1
    ms = jnp.mean(h1 * h1, axis=-1, keepdims=True)
    xn = h1 * lax.rsqrt(ms + EPS) * g_ref[...]
    xn_ref[...] = xn

    logits = _dot(xn, wr_ref[...], HIGHEST) + br_ref[...]
    lane_i = lax.broadcasted_iota(jnp.int32, logits.shape, 1)
    lane = lane_i.astype(F32)

    def first_argmax(vals, top):
        return jnp.min(jnp.where(vals == top, lane, float(LANES)), axis=-1, keepdims=True)

    is_group = lane_i < N_GROUPS
    lg = jnp.where(is_group, logits, NEG_INF)
    mg = jnp.max(lg, axis=-1, keepdims=True)
    grp = first_argmax(lg, mg)
    gp = 1.0 / jnp.sum(jnp.where(is_group, jnp.exp(lg - mg), 0.0), axis=-1, keepdims=True)

    group_of_lane = lax.shift_right_arithmetic(lane_i - N_GROUPS, 3).astype(F32)
    le = jnp.where(group_of_lane == grp, logits, NEG_INF)
    m1 = jnp.max(le, axis=-1, keepdims=True)
    i1 = first_argmax(le, m1)
    le2 = jnp.where(lane == i1, NEG_INF, le)
    m2 = jnp.max(le2, axis=-1, keepdims=True)
    i2 = first_argmax(le2, m2)
    e2 = jnp.exp(m2 - m1)
    w1 = gp / (1.0 + e2)
    w2 = gp * e2 / (1.0 + e2)
    ids = jnp.where(lane_i == 0, i1 - N_GROUPS, jnp.where(lane_i == 1, i2 - N_GROUPS, 0.0))
    ei_ref[...] = ids.astype(jnp.int32)
    ew_ref[...] = jnp.where(lane_i == 0, w1, jnp.where(lane_i == 1, w2, 0.0))


def outproj_router(h, y_rw, y_da, w_a, w_b, ln_g, wr, br, tm=256):
    n, d = h.shape
    row = lambda i: (i, 0)
    const = lambda i: (0, 0)
    return pl.pallas_call(
        _outproj_router_kernel,
        grid=(n // tm,),
        in_specs=[
            pl.BlockSpec((tm, d), row),
            pl.BlockSpec((tm, RW_WIDTH), row),
            pl.BlockSpec((tm, DA_WIDTH), row),
            pl.BlockSpec((RW_WIDTH, d), const),
            pl.BlockSpec((DA_WIDTH, d), const),
            pl.BlockSpec((1, d), const),
            pl.BlockSpec((d, LANES), const),
            pl.BlockSpec((1, LANES), const),
        ],
        out_specs=[pl.BlockSpec((tm, d), row), pl.BlockSpec((tm, d), row),
                   pl.BlockSpec((tm, LANES), row), pl.BlockSpec((tm, LANES), row)],
        out_shape=[jax.ShapeDtypeStruct((n, d), F32), jax.ShapeDtypeStruct((n, d), F32),
                   jax.ShapeDtypeStruct((n, LANES), jnp.int32), jax.ShapeDtypeStruct((n, LANES), F32)],
        compiler_params=_cparams(("parallel",)),
        name="outproj_router",
    )(h, y_rw, y_da, w_a, w_b, ln_g.reshape(1, d), wr, br)


def _moe_kernel(te_ref, nv_ref, tok_ref, dst_ref, x_hbm, rw_ref, wg_ref, wu_ref, wd_ref, out_hbm,
                xbuf, obuf, wg_bf, wu_bf, wd_bf, sem_in, sem_out):
    t = pl.program_id(0)
    rows = MOE_TILE

    def in_copy(r):
        return pltpu.make_async_copy(x_hbm.at[pl.ds(tok_ref[0, 0, r], 1)], xbuf.at[pl.ds(r, 1)], sem_in)

    def out_copy(r):
        return pltpu.make_async_copy(obuf.at[pl.ds(r, 1)], out_hbm.at[pl.ds(dst_ref[0, 0, r], 1)], sem_out)

    @pl.when(t == 0)
    def _():
        n_real = out_hbm.shape[0] - rows
        obuf[...] = jnp.zeros_like(obuf)
        fill = pltpu.make_async_copy(obuf, out_hbm.at[pl.ds(n_real, rows)], sem_out)
        fill.start()
        fill.wait()

    @pl.when(t < nv_ref[0])
    def _():
        def start_in(r, carry):
            in_copy(r).start()
            return carry
        lax.fori_loop(0, rows, start_in, 0)

        new_expert = jnp.logical_or(t == 0, te_ref[t] != te_ref[jnp.maximum(t - 1, 0)])

        @pl.when(new_expert)
        def _():
            wg_bf[...] = wg_ref[0].astype(BF16)
            wu_bf[...] = wu_ref[0].astype(BF16)
            wd_bf[...] = wd_ref[0].astype(BF16)

        def wait_in(r, carry):
            in_copy(r).wait()
            return carry
        lax.fori_loop(0, rows, wait_in, 0)

        x = xbuf[...].astype(BF16)
        hg = _dot(x, wg_bf[...])
        hu = _dot(x, wu_bf[...])
        act = (hg * _sigmoid(hg) * hu).astype(BF16)
        obuf[...] = _dot(act, wd_bf[...]) * rw_ref[...]

        def start_out(r, carry):
            out_copy(r).start()
            return carry
        lax.fori_loop(0, rows, start_out, 0)

        def wait_out(r, carry):
            out_copy(r).wait()
            return carry
        lax.fori_loop(0, rows, wait_out, 0)


def moe_experts(xn, tile_expert, n_valid, row_token, row_dst, row_w, w_gate, w_up, w_down):
    n, d = xn.shape
    nt = tile_expert.shape[0]
    rows = MOE_TILE
    grid_spec = pltpu.PrefetchScalarGridSpec(
        num_scalar_prefetch=2,
        grid=(nt,),
        in_specs=[
            pl.BlockSpec((1, 1, rows), lambda t, te, nv: (t, 0, 0), memory_space=pltpu.SMEM),
            pl.BlockSpec((1, 1, rows), lambda t, te, nv: (t, 0, 0), memory_space=pltpu.SMEM),
            pl.BlockSpec(memory_space=pl.ANY),
            pl.BlockSpec((rows, 1), lambda t, te, nv: (t, 0)),
            pl.BlockSpec((1, d, D_EXPERT), lambda t, te, nv: (te[t], 0, 0)),
            pl.BlockSpec((1, d, D_EXPERT), lambda t, te, nv: (te[t], 0, 0)),
            pl.BlockSpec((1, D_EXPERT, d), lambda t, te, nv: (te[t], 0, 0)),
        ],
        out_specs=pl.BlockSpec(memory_space=pl.ANY),
        scratch_shapes=[
            pltpu.VMEM((rows, d), F32), pltpu.VMEM((rows, d), F32),
            pltpu.VMEM((d, D_EXPERT), BF16), pltpu.VMEM((d, D_EXPERT), BF16),
            pltpu.VMEM((D_EXPERT, d), BF16),
            pltpu.SemaphoreType.DMA(()), pltpu.SemaphoreType.DMA(()),
        ],
    )
    return pl.pallas_call(
        _moe_kernel,
        grid_spec=grid_spec,
        out_shape=jax.ShapeDtypeStruct((2 * n + rows, d), F32),
        compiler_params=_cparams(("arbitrary",)),
        name="moe_experts",
    )(tile_expert, n_valid, row_token.reshape(nt, 1, rows), row_dst.reshape(nt, 1, rows),
      xn, row_w.reshape(nt * rows, 1), w_gate, w_up, w_down)


def moe_plan(eidx, ew, n):
    rows = MOE_TILE
    nt = (2 * n + N_EXPERTS * (rows - 1)) // rows + 1
    e_flat = jnp.concatenate([eidx[:, 0], eidx[:, 1]])
    w_flat = jnp.concatenate([ew[:, 0], ew[:, 1]])
    order = jnp.argsort(e_flat, stable=True).astype(jnp.int32)
    counts = jnp.sum(jax.nn.one_hot(e_flat, N_EXPERTS, dtype=jnp.int32), axis=0)
    starts = jnp.cumsum(counts) - counts
    pcounts = ((counts + rows - 1) // rows) * rows
    pends = jnp.cumsum(pcounts)
    pstarts = pends - pcounts
    n_valid = (pends[-1] // rows).astype(jnp.int32)
    tile_start = jnp.arange(nt, dtype=jnp.int32) * rows
    tile_expert = jnp.searchsorted(pends, tile_start, side="right").astype(jnp.int32)
    last_expert = tile_expert[jnp.maximum(n_valid - 1, 0)]
    tile_expert = jnp.where(jnp.arange(nt) < n_valid, jnp.minimum(tile_expert, N_EXPERTS - 1), last_expert)
    r = jnp.arange(nt * rows, dtype=jnp.int32)
    e_r = tile_expert[r // rows]
    local = r - pstarts[e_r]
    valid = jnp.logical_and(local < counts[e_r], r < pends[-1])
    src = order[jnp.clip(starts[e_r] + local, 0, 2 * n - 1)]
    row_token = jnp.where(valid, src % n, 0).astype(jnp.int32)
    row_dst = jnp.where(valid, src, 2 * n + r % rows).astype(jnp.int32)
    row_w = jnp.where(valid, w_flat[src], 0.0).astype(F32)
    return tile_expert, n_valid.reshape(1), row_token, row_dst, row_w


def _ple_kernel(h_ref, ya_ref, yb_ref, p_ref, wg_ref, bg_ref, wp_ref, o_ref):
    h2 = h_ref[...] + ya_ref[...] + yb_ref[...]
    z = _dot(h2.astype(BF16), wg_ref[...]) + bg_ref[...]
    pp = _dot(p_ref[...].astype(BF16), wp_ref[...])
    o_ref[...] = h2 + _sigmoid(z) * pp


def ple(h1, y2, p, wg, bg, wp, tm=256):
    n, d = h1.shape
    row = lambda i: (i, 0)
    const = lambda i: (0, 0)
    nb = n // tm
    return pl.pallas_call(
        _ple_kernel,
        grid=(nb,),
        in_specs=[
            pl.BlockSpec((tm, d), row),
            pl.BlockSpec((tm, d), row),
            pl.BlockSpec((tm, d), lambda i: (i + nb, 0)),
            pl.BlockSpec((tm, PLE_DIM), row),
            pl.BlockSpec((d, d), const),
            pl.BlockSpec((1, d), const),
            pl.BlockSpec((PLE_DIM, d), const),
        ],
        out_specs=pl.BlockSpec((tm, d), row),
        out_shape=jax.ShapeDtypeStruct((n, d), F32),
        compiler_params=_cparams(("parallel",)),
        name="ple",
    )(h1, y2, y2, p, wg, bg.reshape(1, d), wp)


def _pad_rows(w, rows, offset=0):
    return jnp.zeros((rows, w.shape[1]), F32).at[offset:offset + w.shape[0]].set(w)


def _rope_tables(seq):
    pos = jnp.arange(seq, dtype=F32)
    inv_freq = ROPE_THETA ** (-(jnp.arange(0, ROPE_DIMS, 2, dtype=F32) / ROPE_DIMS))
    ang = pos[:, None] * inv_freq[None, :]
    cos, sin = jnp.cos(ang), jnp.sin(ang)
    half = ROPE_DIMS // 2
    rest = DA_HD - ROPE_DIMS
    one = jnp.ones((seq, rest), F32)
    zero = jnp.zeros((seq, rest), F32)
    zh = jnp.zeros((seq, half), F32)
    cos_t = jnp.concatenate([cos, cos, one], axis=1)
    sa_t = jnp.concatenate([-sin, zh, zero], axis=1)
    sb_t = jnp.concatenate([zh, sin, zero], axis=1)
    tile2 = lambda t: jnp.concatenate([t, t], axis=1)
    return tile2(cos_t), tile2(sa_t), tile2(sb_t)


def kernel(x, p, ln1_g, ln2_g, w_in_first, w_in_rest, mu_first, mu_rest, rw_w0, rw_w2, rw_a0, rw_a2, rw_g2, rw_v0, rw_v2, rw_k_k, rw_k_a, rw_r_k, rw_gn_g, rw_gn_b, da_q_g, da_k_g, da_lq1, da_lk1, da_lq2, da_lk2, da_subln_g, w_out, moe_wr_group, moe_br_group, moe_wr_expert, moe_br_expert, moe_w_gate, moe_w_up, moe_w_down, ple_w_gate, ple_b_gate, ple_w_proj):
    batch, seq, d = x.shape
    n = batch * seq
    depth = p.shape[0]
    c = RW_WIDTH
    cos_t, sa_t, sb_t = _rope_tables(seq)

    h = x.reshape(n, d)
    v_first = None
    for i in range(depth):
        if i == 0:
            w_in, mu, n_rw = w_in_first, mu_first, 3 * c + RW_DECAY_LORA + RW_AAA_LORA + RW_GATE_LORA
        else:
            w_in, mu, n_rw = w_in_rest[i - 1], mu_rest[i - 1], 3 * c + RW_DECAY_LORA + RW_AAA_LORA + RW_GATE_LORA + RW_MV_LORA
        n_lora = n_rw - 3 * c
        w_rw = jnp.pad(w_in[:, :n_rw], ((0, 0), (0, LORA_PAD - n_lora))).astype(BF16)
        w_da = w_in[:, n_rw:].astype(BF16)
        mu_p = jnp.pad(mu, (0, LORA_PAD - n_lora)).reshape(1, RW_COLS_PAD)

        proj_rw = norm_matmul(h, ln1_g[i], w_rw, 512, RW_COLS_PAD // 3)
        proj_da = norm_matmul(h, ln1_g[i], w_da, 512, DA_WIDTH)

        zero_c = jnp.zeros((c,), F32)
        vec = jnp.stack([rw_w0[i], rw_a0[i], rw_v0[i - 1] if i > 0 else zero_c, rw_k_k[i], rw_k_a[i],
                         zero_c, zero_c, zero_c])
        w2p = _pad_rows(rw_w2[i], LANES, 0)
        a2p = _pad_rows(rw_a2[i], LANES, RW_DECAY_LORA)
        g2p = _pad_rows(rw_g2[i], 2 * LANES, 0)
        v2p = _pad_rows(rw_v2[i - 1], 2 * LANES, RW_GATE_LORA) if i > 0 else None
        r, lw, k, v, kk, a, g = rwkv_prep(proj_rw, mu_p, vec, w2p, a2p, g2p, v2p, v_first, seq)
        if i == 0:
            v_first = v
        par = jnp.stack([rw_r_k[i].reshape(c), rw_gn_g[i], rw_gn_b[i]] + [zero_c] * 5)
        y_rw = wkv(r, lw, k, v, kk, a, g, par, batch, seq)

        lambda_init = 0.8 - 0.6 * math.exp(-0.3 * i)
        tile2 = lambda t: jnp.concatenate([t, t])
        zl = jnp.zeros((LANES,), F32)
        gains = jnp.stack([tile2(da_q_g[i]), tile2(da_k_g[i])] + [zl] * 6)
        q, kd, vd = attn_prep(proj_da, cos_t, sa_t, sb_t, gains, seq)
        pad64 = lambda t: jnp.pad(t, (0, LANES - DA_HD))
        apar = jnp.stack([pad64(da_lq1[i]), pad64(da_lk1[i]), pad64(da_lq2[i]), pad64(da_lk2[i]),
                          da_subln_g[i], zl, zl, zl])
        y_da = flash_diff_attention(q, kd, vd, apar, batch, seq, lambda_init)

        w_o = w_out[i].astype(BF16)
        wr = jnp.pad(jnp.concatenate([moe_wr_group[i], moe_wr_expert[i]], axis=1),
                     ((0, 0), (0, LANES - N_GROUPS - N_EXPERTS)))
        br = jnp.pad(jnp.concatenate([moe_br_group[i], moe_br_expert[i]]),
                     (0, LANES - N_GROUPS - N_EXPERTS)).reshape(1, LANES)
        h1, xn2, eidx, ew = outproj_router(h, y_rw, y_da, w_o[:c], w_o[c:], ln2_g[i], wr, br)

        tile_expert, n_valid, row_token, row_dst, row_w = moe_plan(eidx, ew, n)
        y2 = moe_experts(xn2, tile_expert, n_valid, row_token, row_dst, row_w,
                         moe_w_gate[i], moe_w_up[i], moe_w_down[i])

        h = ple(h1, y2, p[i].reshape(n, PLE_DIM), ple_w_gate[i].astype(BF16), ple_b_gate[i],
                ple_w_proj[i].astype(BF16))
    return h.reshape(batch, seq, d)
```
